```python
import jax, jax.numpy as jnp
from jax import lax
import numpy as np

D_MODEL = 2048
BATCH = 4
SEQ = 2048
DEPTH = 1
DEC_BATCH = 8
DEC_SEQ = 32
PAST_LEN = 2048

CHUNK = 64
MIX_WIDTH = D_MODEL
SB_WIDTH = MIX_WIDTH // 2
SB_HEAD_DIM = 128
SB_HEADS = SB_WIDTH // SB_HEAD_DIM
POOL_WIDTH = MIX_WIDTH - SB_WIDTH
POOL_WINDOWS = (2, 4, 8, 16)
POOL_GROUPS = len(POOL_WINDOWS)
POOL_GROUP_DIM = POOL_WIDTH // POOL_GROUPS
POOL_STATE = max(POOL_WINDOWS) - 1
IN_WIDTH = 3 * SB_WIDTH + POOL_WIDTH
D_FF = 5632
PLE_DIM = 256
Q_BLOCK = 128
RMS_EPS = 1e-6
SB_SCALE = SB_HEAD_DIM ** -0.5

kernel_name = "hymba_stickbreak_pool_macaron_stream_step"


def _rmsnorm(x, g):
    xf = x.astype(jnp.float32)
    r = lax.rsqrt(jnp.mean(xf * xf, axis=-1, keepdims=True) + RMS_EPS)
    return (xf * r * g.astype(jnp.float32)).astype(x.dtype)


def _swiglu(x, w_gate, w_up, w_down):
    return (jax.nn.silu(x @ w_gate) * (x @ w_up)) @ w_down


def _sb_block(qb, qpos, k, v, kpos):
    z = jnp.einsum('bqhd,bkhd->bhqk', qb, k).astype(jnp.float32) * SB_SCALE
    mask = kpos[None, :] < qpos[:, None]
    log_beta = jax.nn.log_sigmoid(z)
    log_rest = jnp.where(mask, log_beta - z, 0.0)
    suffix = lax.cumsum(log_rest, axis=3, reverse=True) - log_rest
    a = jnp.where(mask, jnp.exp(log_beta + suffix), 0.0)
    return jnp.einsum('bhqk,bkhd->bqhd', a, v.astype(jnp.float32))


def _stick_breaking(q, k, v, qpos, kpos):
    b, t, h, d = q.shape
    if t <= Q_BLOCK:
        return _sb_block(q, qpos, k, v, kpos)
    nb = t // Q_BLOCK
    qb = q.reshape(b, nb, Q_BLOCK, h, d).transpose(1, 0, 2, 3, 4)
    pb = qpos.reshape(nb, Q_BLOCK)
    out = lax.map(lambda args: _sb_block(args[0], args[1], k, v, kpos), (qb, pb))
    return out.transpose(1, 0, 2, 3, 4).reshape(b, t, h, d)


def _multiscale_pool(xp, pool_past, pos, w_pool):
    b, t, _ = xp.shape
    full = jnp.concatenate([pool_past.astype(xp.dtype), xp], axis=1)
    ff = full.astype(jnp.float32)
    csum = jnp.concatenate([jnp.zeros((b, 1, POOL_WIDTH), jnp.float32), jnp.cumsum(ff, axis=1)], axis=1)
    xf = xp.astype(jnp.float32)
    outs = []
    for g, w in enumerate(POOL_WINDOWS):
        sl = slice(g * POOL_GROUP_DIM, (g + 1) * POOL_GROUP_DIM)
        cg = csum[:, :, sl]
        win = cg[:, POOL_STATE + 1:POOL_STATE + 1 + t] - cg[:, POOL_STATE + 1 - w:POOL_STATE + 1 - w + t]
        cnt = jnp.minimum(pos + 1, w).astype(jnp.float32)
        outs.append(win / cnt[None, :, None] - xf[:, :, sl])
    pooled = jnp.stack(outs, axis=2)
    mixed = jnp.einsum('btgc,gcd->btgd', pooled, w_pool.astype(jnp.float32))
    return mixed.reshape(b, t, POOL_WIDTH).astype(xp.dtype), full[:, -POOL_STATE:]


def _layer(x, p, pos, k_past, v_past, pool_past, g_ffn1, w1_gate, w1_up, w1_down, g_mix, w_in,
           g_attn_out, w_pool, pool_scale, w_out, g_ffn2, w2_gate, w2_up, w2_down,
           g_ple, w_ple_gate, w_ple_proj):
    b, t, _ = x.shape
    h = x + 0.5 * _swiglu(_rmsnorm(x, g_ffn1), w1_gate, w1_up, w1_down)
    u = _rmsnorm(h, g_mix)
    z = u @ w_in
    q, k, v, xp = jnp.split(z, [SB_WIDTH, 2 * SB_WIDTH, 3 * SB_WIDTH], axis=-1)
    q = q.reshape(b, t, SB_HEADS, SB_HEAD_DIM)
    k = k.reshape(b, t, SB_HEADS, SB_HEAD_DIM)
    v = v.reshape(b, t, SB_HEADS, SB_HEAD_DIM)
    if k_past is None:
        k_all, v_all = k, v
    else:
        k_all = jnp.concatenate([k_past.astype(k.dtype), k], axis=1)
        v_all = jnp.concatenate([v_past.astype(v.dtype), v], axis=1)
    kpos = jnp.arange(k_all.shape[1], dtype=jnp.int32)
    attn = _stick_breaking(q, k_all, v_all, pos, kpos).reshape(b, t, SB_WIDTH)
    attn = _rmsnorm(attn.astype(x.dtype), g_attn_out)
    if pool_past is None:
        pool_past = jnp.zeros((b, POOL_STATE, POOL_WIDTH), xp.dtype)
    pooled, new_pool = _multiscale_pool(xp, pool_past, pos, w_pool)
    pooled = _rmsnorm(pooled, pool_scale)
    h = h + jnp.concatenate([attn, pooled], axis=-1) @ w_out
    h = h + 0.5 * _swiglu(_rmsnorm(h, g_ffn2), w2_gate, w2_up, w2_down)
    gate = jax.nn.sigmoid((_rmsnorm(h, g_ple) @ w_ple_gate).astype(jnp.float32))
    h = h + (gate * (p @ w_ple_proj).astype(jnp.float32)).astype(h.dtype)
    return h, k, v, new_pool


def setup_inputs(seed: int = 0) -> dict:
    key = jax.random.key(seed)
    ks = jax.random.split(key, 28)
    f32 = jnp.float32

    def nrm(k, shape, scale):
        return jax.random.normal(k, shape, f32) * scale

    def gain(k, shape):
        return 1.0 + 0.1 * jax.random.normal(k, shape, f32)

    D = D_MODEL
    return {
        "x_prompt": nrm(ks[0], (BATCH, SEQ, D), 1.0),
        "x_sample": nrm(ks[1], (DEC_BATCH, DEC_SEQ, D), 1.0),
        "cache_k": nrm(ks[2], (DEPTH, DEC_BATCH, PAST_LEN, SB_HEADS, SB_HEAD_DIM), 1.0),
        "cache_v": nrm(ks[3], (DEPTH, DEC_BATCH, PAST_LEN, SB_HEADS, SB_HEAD_DIM), 1.0),
        "state_pool": nrm(ks[4], (DEPTH, DEC_BATCH, POOL_STATE, POOL_WIDTH), 1.0),
        "p_prompt": nrm(ks[5], (DEPTH, BATCH, SEQ, PLE_DIM), 1.0),
        "p_sample": nrm(ks[6], (DEPTH, DEC_BATCH, DEC_SEQ, PLE_DIM), 1.0),
        "g_ffn1": gain(ks[7], (DEPTH, D)),
        "w1_gate": nrm(ks[8], (DEPTH, D, D_FF), D ** -0.5),
        "w1_up": nrm(ks[9], (DEPTH, D, D_FF), D ** -0.5),
        "w1_down": nrm(ks[10], (DEPTH, D_FF, D), D_FF ** -0.5),
        "g_mix": gain(ks[11], (DEPTH, D)),
        "w_in": nrm(ks[12], (DEPTH, D, IN_WIDTH), D ** -0.5),
        "g_attn_out": gain(ks[13], (DEPTH, SB_WIDTH)),
        "w_pool": nrm(ks[14], (DEPTH, POOL_GROUPS, POOL_GROUP_DIM, POOL_GROUP_DIM), POOL_GROUP_DIM ** -0.5),
        "pool_scale": gain(ks[15], (DEPTH, POOL_WIDTH)),
        "w_out": nrm(ks[16], (DEPTH, MIX_WIDTH, D), MIX_WIDTH ** -0.5),
        "g_ffn2": gain(ks[17], (DEPTH, D)),
        "w2_gate": nrm(ks[18], (DEPTH, D, D_FF), D ** -0.5),
        "w2_up": nrm(ks[19], (DEPTH, D, D_FF), D ** -0.5),
        "w2_down": nrm(ks[20], (DEPTH, D_FF, D), D_FF ** -0.5),
        "g_ple": gain(ks[21], (DEPTH, D)),
        "w_ple_gate": nrm(ks[22], (DEPTH, D, D), D ** -0.5),
        "w_ple_proj": nrm(ks[23], (DEPTH, PLE_DIM, D), PLE_DIM ** -0.5),
        "g_final": gain(ks[24], (D,)),
    }


def reference(x_prompt, x_sample, cache_k, cache_v, state_pool, p_prompt, p_sample,
              g_ffn1, w1_gate, w1_up, w1_down, g_mix, w_in, g_attn_out, w_pool, pool_scale,
              w_out, g_ffn2, w2_gate, w2_up, w2_down, g_ple, w_ple_gate, w_ple_proj, g_final):
    t_p = x_prompt.shape[1]
    t_s = x_sample.shape[1]
    past = cache_k.shape[2]
    pos_p = jnp.arange(t_p, dtype=jnp.int32)
    pos_s = past + jnp.arange(t_s, dtype=jnp.int32)
    hp, hs = x_prompt, x_sample
    kp_l, vp_l, sp_l, ks_l, vs_l, ss_l = [], [], [], [], [], []
    for i in range(DEPTH):
        wts = (g_ffn1[i], w1_gate[i], w1_up[i], w1_down[i], g_mix[i], w_in[i], g_attn_out[i],
               w_pool[i], pool_scale[i], w_out[i], g_ffn2[i], w2_gate[i], w2_up[i], w2_down[i],
               g_ple[i], w_ple_gate[i], w_ple_proj[i])
        hp, kp, vp, sp = _layer(hp, p_prompt[i], pos_p, None, None, None, *wts)
        hs, ks_, vs_, ss = _layer(hs, p_sample[i], pos_s, cache_k[i], cache_v[i], state_pool[i], *wts)
        kp_l.append(kp); vp_l.append(vp); sp_l.append(sp)
        ks_l.append(ks_); vs_l.append(vs_); ss_l.append(ss)
    y_prompt = _rmsnorm(hp, g_final)
    y_sample = _rmsnorm(hs, g_final)
    new_k_prompt = jnp.stack(kp_l)
    new_v_prompt = jnp.stack(vp_l)
    new_pool_prompt = jnp.stack(sp_l)
    new_k_sample = jnp.stack(ks_l)
    new_v_sample = jnp.stack(vs_l)
    new_pool_sample = jnp.stack(ss_l)
    return (y_prompt, y_sample, new_k_prompt, new_v_prompt, new_pool_prompt, new_k_sample, new_v_sample, new_pool_sample)
```

```python
import functools

import jax
import jax.numpy as jnp
from jax import lax
from jax.experimental import pallas as pl
from jax.experimental.pallas import tpu as pltpu

RMS_EPS = 1e-6
SB_HEAD_DIM = 128
POOL_WINDOWS = (2, 4, 8, 16)
POOL_STATE = max(POOL_WINDOWS) - 1
POOL_HALO = 16
SB_SCALE = SB_HEAD_DIM ** -0.5

V7X_VMEM_LIMIT_BYTES = 60 * 1024 * 1024

F32 = jnp.float32
BF16 = jnp.bfloat16


def _params(n_axes):
    return pltpu.CompilerParams(dimension_semantics=("arbitrary",) * n_axes,
                                vmem_limit_bytes=V7X_VMEM_LIMIT_BYTES)


def _rms(x, g):
    r = lax.rsqrt(jnp.mean(x * x, axis=-1, keepdims=True) + RMS_EPS)
    return x * r * g


def _row_tile(m, target):
    t = min(m, target)
    assert m % t == 0, (m, t)
    return t


def _ffn_kernel(x_ref, g_ref, wg_ref, wu_ref, wd_ref, o_ref, n_ref):
    @pl.when(pl.program_id(1) == 0)
    def _():
        x = x_ref[...]
        n_ref[...] = _rms(x, g_ref[...]).astype(BF16)
        o_ref[...] = x

    n = n_ref[...]
    gate = jnp.dot(n, wg_ref[...], preferred_element_type=F32)
    up = jnp.dot(n, wu_ref[...], preferred_element_type=F32)
    act = (gate * jax.nn.sigmoid(gate) * up).astype(BF16)
    o_ref[...] += 0.5 * jnp.dot(act, wd_ref[...], preferred_element_type=F32)


def _ffn(x, g, wg, wu, wd, *, tm_target=512, tf=512):
    m, d = x.shape
    f = wg.shape[1]
    tm = _row_tile(m, tm_target)
    assert f % tf == 0
    return pl.pallas_call(
        _ffn_kernel,
        grid=(m // tm, f // tf),
        in_specs=[
            pl.BlockSpec((tm, d), lambda i, j: (i, 0)),
            pl.BlockSpec((1, d), lambda i, j: (0, 0)),
            pl.BlockSpec((d, tf), lambda i, j: (0, j)),
            pl.BlockSpec((d, tf), lambda i, j: (0, j)),
            pl.BlockSpec((tf, d), lambda i, j: (j, 0)),
        ],
        out_specs=pl.BlockSpec((tm, d), lambda i, j: (i, 0)),
        out_shape=jax.ShapeDtypeStruct((m, d), F32),
        scratch_shapes=[pltpu.VMEM((tm, d), BF16)],
        compiler_params=_params(2),
        name="ffn",
    )(x, g, wg, wu, wd)


def _inproj_kernel(h_ref, g_ref, w_ref, q_ref, k_ref, v_ref, xp_ref, n_ref):
    j = pl.program_id(1)

    @pl.when(j == 0)
    def _():
        n_ref[...] = _rms(h_ref[...], g_ref[...]).astype(BF16)

    z = jnp.dot(n_ref[...], w_ref[...], preferred_element_type=F32)

    @pl.when(j == 0)
    def _():
        q_ref[...] = z.astype(BF16)

    @pl.when(j == 1)
    def _():
        k_ref[...] = z

    @pl.when(j == 2)
    def _():
        v_ref[...] = z

    @pl.when(j == 3)
    def _():
        xp_ref[...] = z


def _inproj(h, g, w_in, *, tm_target=512):
    m, d = h.shape
    n_total = w_in.shape[1]
    tn = n_total // 4
    tm = _row_tile(m, tm_target)
    row_panel = lambda i, j: (i, 0)
    return pl.pallas_call(
        _inproj_kernel,
        grid=(m // tm, 4),
        in_specs=[
            pl.BlockSpec((tm, d), row_panel),
            pl.BlockSpec((1, d), lambda i, j: (0, 0)),
            pl.BlockSpec((d, tn), lambda i, j: (0, j)),
        ],
        out_specs=[pl.BlockSpec((tm, tn), row_panel)] * 4,
        out_shape=[jax.ShapeDtypeStruct((m, tn), BF16)] + [jax.ShapeDtypeStruct((m, tn), F32)] * 3,
        scratch_shapes=[pltpu.VMEM((tm, d), BF16)],
        compiler_params=_params(2),
        name="inproj",
    )(h, g, w_in)


def _suffix_matrix(tk):
    row = lax.broadcasted_iota(jnp.int32, (tk, tk), 0)
    col = lax.broadcasted_iota(jnp.int32, (tk, tk), 1)
    return (row >= col).astype(BF16)


def _sb_block(q, kb, vb, carry, suffix, mask):
    z = lax.dot_general(q, kb, (((1,), (1,)), ((), ())), preferred_element_type=F32) * SB_SCALE
    sp = jnp.maximum(z, 0.0) + jnp.log1p(jnp.exp(-jnp.abs(z)))
    if mask is not None:
        sp = jnp.where(mask, sp, 0.0)
    hi = sp.astype(BF16)
    lo = (sp - hi.astype(F32)).astype(BF16)
    csum = (jnp.dot(hi, suffix, preferred_element_type=F32)
            + jnp.dot(lo, suffix, preferred_element_type=F32))
    w = jnp.exp(z - (csum + carry))
    if mask is not None:
        w = jnp.where(mask, w, 0.0)
    pv = jnp.dot(w.astype(BF16), vb, preferred_element_type=F32)
    return pv, carry + csum[:, 0:1]


def _attn_kernel(*refs, tq, tkp, n_past_blocks):
    if n_past_blocks:
        q_ref, k_ref, v_ref, kp_ref, vp_ref, o_ref, kb_ref, vb_ref, kpb_ref, vpb_ref = refs
    else:
        q_ref, k_ref, v_ref, o_ref, kb_ref, vb_ref = refs
    i = pl.program_id(2)

    @pl.when(i == 0)
    def _():
        kb_ref[...] = k_ref[0].astype(BF16)
        vb_ref[...] = v_ref[0].astype(BF16)
        if n_past_blocks:
            kpb_ref[...] = kp_ref[0].astype(BF16)
            vpb_ref[...] = vp_ref[0].astype(BF16)

    q = q_ref[0]
    row = lax.broadcasted_iota(jnp.int32, (tq, tq), 0)
    col = lax.broadcasted_iota(jnp.int32, (tq, tq), 1)
    start = pl.multiple_of(i * tq, tq)
    acc, carry = _sb_block(q, kb_ref[pl.ds(start, tq), :], vb_ref[pl.ds(start, tq), :],
                           jnp.zeros((tq, 1), F32), _suffix_matrix(tq), col < row)

    def new_body(it, state):
        acc, carry = state
        s = pl.multiple_of((i - 1 - it) * tq, tq)
        pv, carry = _sb_block(q, kb_ref[pl.ds(s, tq), :], vb_ref[pl.ds(s, tq), :], carry,
                              _suffix_matrix(tq), None)
        return acc + pv, carry

    acc, carry = lax.fori_loop(0, i, new_body, (acc, carry))

    if n_past_blocks:
        def past_body(it, state):
            acc, carry = state
            s = pl.multiple_of((n_past_blocks - 1 - it) * tkp, tkp)
            pv, carry = _sb_block(q, kpb_ref[pl.ds(s, tkp), :], vpb_ref[pl.ds(s, tkp), :], carry,
                                  _suffix_matrix(tkp), None)
            return acc + pv, carry

        acc, carry = lax.fori_loop(0, n_past_blocks, past_body, (acc, carry))

    o_ref[0] = acc


def _attention(q, k, v, k_past=None, v_past=None, *, tq_target=256, tkp_target=256):
    b, t, w = q.shape
    heads = w // SB_HEAD_DIM
    tq = _row_tile(t, tq_target)
    hd = SB_HEAD_DIM
    in_specs = [
        pl.BlockSpec((1, tq, hd), lambda bi, h, i: (bi, i, h)),
        pl.BlockSpec((1, t, hd), lambda bi, h, i: (bi, 0, h)),
        pl.BlockSpec((1, t, hd), lambda bi, h, i: (bi, 0, h)),
    ]
    scratch = [pltpu.VMEM((t, hd), BF16), pltpu.VMEM((t, hd), BF16)]
    args = [q, k, v]
    n_past_blocks = 0
    tkp = 0
    if k_past is not None:
        p = k_past.shape[1]
        tkp = _row_tile(p, tkp_target)
        n_past_blocks = p // tkp
        in_specs += [pl.BlockSpec((1, p, hd), lambda bi, h, i: (bi, 0, h))] * 2
        scratch += [pltpu.VMEM((p, hd), BF16), pltpu.VMEM((p, hd), BF16)]
        args += [k_past, v_past]
    return pl.pallas_call(
        functools.partial(_attn_kernel, tq=tq, tkp=tkp, n_past_blocks=n_past_blocks),
        grid=(b, heads, t // tq),
        in_specs=in_specs,
        out_specs=pl.BlockSpec((1, tq, hd), lambda bi, h, i: (bi, i, h)),
        out_shape=jax.ShapeDtypeStruct((b, t, w), F32),
        scratch_shapes=scratch,
        compiler_params=_params(3),
        name="sb_attn",
    )(*args)


def _pool_kernel(xp_ref, halo_ref, past_ref, wp_ref, ps_ref, o_ref, state_ref, buf_ref, *, tt, pos0, n_tiles):
    i = pl.program_id(1)
    x = xp_ref[0]
    width = x.shape[-1]
    gdim = width // len(POOL_WINDOWS)
    buf_ref[POOL_HALO:POOL_HALO + tt, :] = x

    @pl.when(i == 0)
    def _():
        buf_ref[0:POOL_HALO - POOL_STATE, :] = jnp.zeros((POOL_HALO - POOL_STATE, width), F32)
        buf_ref[POOL_HALO - POOL_STATE:POOL_HALO, :] = past_ref[0]

    @pl.when(i > 0)
    def _():
        buf_ref[0:POOL_HALO, :] = halo_ref[0]

    pos = pos0 + i * tt + lax.broadcasted_iota(jnp.int32, (tt, 1), 0)
    mixed = []
    for g, win in enumerate(POOL_WINDOWS):
        cols = slice(g * gdim, (g + 1) * gdim)
        total = x[:, cols]
        for back in range(1, win):
            total = total + buf_ref[POOL_HALO - back:POOL_HALO - back + tt, cols]
        cnt = jnp.minimum(pos + 1, win).astype(F32)
        pooled = total / cnt - x[:, cols]
        mixed.append(jnp.dot(pooled.astype(BF16), wp_ref[g], preferred_element_type=F32))
    sumsq = sum(jnp.sum(mg * mg, axis=-1, keepdims=True) for mg in mixed)
    r = lax.rsqrt(sumsq / width + RMS_EPS)
    for g, mg in enumerate(mixed):
        cols = slice(g * gdim, (g + 1) * gdim)
        o_ref[0, :, cols] = (mg * r * ps_ref[:, cols]).astype(BF16)

    @pl.when(i == n_tiles - 1)
    def _():
        state_ref[0] = buf_ref[POOL_HALO + tt - POOL_STATE:POOL_HALO + tt, :]


def _pool(xp, past, w_pool, pool_scale, *, pos0, tt_target=512):
    b, t, w = xp.shape
    assert t >= POOL_HALO and t % POOL_HALO == 0
    tt = _row_tile(t, tt_target)
    n_tiles = t // tt
    halo_blocks = tt // POOL_HALO
    return pl.pallas_call(
        functools.partial(_pool_kernel, tt=tt, pos0=pos0, n_tiles=n_tiles),
        grid=(b, n_tiles),
        in_specs=[
            pl.BlockSpec((1, tt, w), lambda bi, i: (bi, i, 0)),
            pl.BlockSpec((1, POOL_HALO, w), lambda bi, i: (bi, jnp.maximum(i * halo_blocks - 1, 0), 0)),
            pl.BlockSpec((1, POOL_STATE, w), lambda bi, i: (bi, 0, 0)),
            pl.BlockSpec(w_pool.shape, lambda bi, i: (0, 0, 0)),
            pl.BlockSpec((1, w), lambda bi, i: (0, 0)),
        ],
        out_specs=[
            pl.BlockSpec((1, tt, w), lambda bi, i: (bi, i, 0)),
            pl.BlockSpec((1, POOL_STATE, w), lambda bi, i: (bi, 0, 0)),
        ],
        out_shape=[jax.ShapeDtypeStruct((b, t, w), BF16), jax.ShapeDtypeStruct((b, POOL_STATE, w), F32)],
        scratch_shapes=[pltpu.VMEM((POOL_HALO + tt, w), F32)],
        compiler_params=_params(2),
        name="pool",
    )(xp, xp, past, w_pool, pool_scale)


def _outproj_kernel(attn_ref, pool_ref, h_ref, ga_ref, w_ref, o_ref):
    sbw = attn_ref.shape[-1]
    an = _rms(attn_ref[...], ga_ref[...]).astype(BF16)
    mix = (jnp.dot(an, w_ref[0:sbw, :], preferred_element_type=F32)
           + jnp.dot(pool_ref[...], w_ref[sbw:, :], preferred_element_type=F32))
    o_ref[...] = h_ref[...] + mix


def _outproj(attn, pooled, h, g_attn, w_out, *, tm_target=512):
    m, d = h.shape
    sbw = attn.shape[1]
    pw = pooled.shape[1]
    tm = _row_tile(m, tm_target)
    return pl.pallas_call(
        _outproj_kernel,
        grid=(m // tm,),
        in_specs=[
            pl.BlockSpec((tm, sbw), lambda i: (i, 0)),
            pl.BlockSpec((tm, pw), lambda i: (i, 0)),
            pl.BlockSpec((tm, d), lambda i: (i, 0)),
            pl.BlockSpec((1, sbw), lambda i: (0, 0)),
            pl.BlockSpec(w_out.shape, lambda i: (0, 0)),
        ],
        out_specs=pl.BlockSpec((tm, d), lambda i: (i, 0)),
        out_shape=jax.ShapeDtypeStruct((m, d), F32),
        compiler_params=_params(1),
        name="outproj",
    )(attn, pooled, h, g_attn, w_out)


def _ple_kernel(h_ref, p_ref, gp_ref, wg_ref, wp_ref, gf_ref, o_ref, *, final_norm):
    h = h_ref[...]
    n = _rms(h, gp_ref[...]).astype(BF16)
    gate = jax.nn.sigmoid(jnp.dot(n, wg_ref[...], preferred_element_type=F32))
    proj = jnp.dot(p_ref[...].astype(BF16), wp_ref[...], preferred_element_type=F32)
    out = h + gate * proj
    if final_norm:
        out = _rms(out, gf_ref[...])
    o_ref[...] = out


def _ple(h, p, g_ple, w_gate, w_proj, g_final, *, final_norm, tm_target=512):
    m, d = h.shape
    pd = p.shape[1]
    tm = _row_tile(m, tm_target)
    return pl.pallas_call(
        functools.partial(_ple_kernel, final_norm=final_norm),
        grid=(m // tm,),
        in_specs=[
            pl.BlockSpec((tm, d), lambda i: (i, 0)),
            pl.BlockSpec((tm, pd), lambda i: (i, 0)),
            pl.BlockSpec((1, d), lambda i: (0, 0)),
            pl.BlockSpec(w_gate.shape, lambda i: (0, 0)),
            pl.BlockSpec(w_proj.shape, lambda i: (0, 0)),
            pl.BlockSpec((1, d), lambda i: (0, 0)),
        ],
        out_specs=pl.BlockSpec((tm, d), lambda i: (i, 0)),
        out_shape=jax.ShapeDtypeStruct((m, d), F32),
        compiler_params=_params(1),
        name="ple",
    )(h, p, g_ple, w_gate, w_proj, g_final)


def _layer(x, p, pos0, k_past, v_past, pool_past, wts, g_final, final_norm):
    b, t, d = x.shape
    m = b * t
    h = _ffn(x.reshape(m, d), wts["g_ffn1"], wts["w1_gate"], wts["w1_up"], wts["w1_down"])
    q, k, v, xp = _inproj(h, wts["g_mix"], wts["w_in"])
    sbw = q.shape[1]
    heads = sbw // SB_HEAD_DIM
    to3 = lambda a: a.reshape(b, t, a.shape[-1])
    if k_past is not None:
        pl_ = k_past.shape[1]
        k_past = k_past.reshape(b, pl_, sbw)
        v_past = v_past.reshape(b, pl_, sbw)
    attn = _attention(to3(q), to3(k), to3(v), k_past, v_past)
    if pool_past is None:
        pool_past = jnp.zeros((b, POOL_STATE, xp.shape[1]), F32)
    pooled, new_pool = _pool(to3(xp), pool_past, wts["w_pool"], wts["pool_scale"], pos0=pos0)
    h = _outproj(attn.reshape(m, sbw), pooled.reshape(m, -1), h, wts["g_attn_out"], wts["w_out"])
    h = _ffn(h, wts["g_ffn2"], wts["w2_gate"], wts["w2_up"], wts["w2_down"])
    h = _ple(h, p.reshape(m, -1), wts["g_ple"], wts["w_ple_gate"], wts["w_ple_proj"], g_final,
             final_norm=final_norm)
    return (h.reshape(b, t, d), k.reshape(b, t, heads, SB_HEAD_DIM), v.reshape(b, t, heads, SB_HEAD_DIM),
            new_pool)


def kernel(x_prompt, x_sample, cache_k, cache_v, state_pool, p_prompt, p_sample, g_ffn1, w1_gate, w1_up, w1_down, g_mix, w_in, g_attn_out, w_pool, pool_scale, w_out, g_ffn2, w2_gate, w2_up, w2_down, g_ple, w_ple_gate, w_ple_proj, g_final):
    depth = w_in.shape[0]
    past = cache_k.shape[2]
    matrices = dict(w1_gate=w1_gate, w1_up=w1_up, w1_down=w1_down, w_in=w_in, w_pool=w_pool, w_out=w_out,
                    w2_gate=w2_gate, w2_up=w2_up, w2_down=w2_down, w_ple_gate=w_ple_gate,
                    w_ple_proj=w_ple_proj)
    gains = dict(g_ffn1=g_ffn1, g_mix=g_mix, g_attn_out=g_attn_out, pool_scale=pool_scale, g_ffn2=g_ffn2,
                 g_ple=g_ple)
    hp, hs = x_prompt, x_sample
    outs = [[] for _ in range(6)]
    gf = g_final.reshape(1, -1)
    for i in range(depth):
        wts = {n: a[i].astype(BF16) for n, a in matrices.items()}
        wts.update({n: a[i].reshape(1, -1) for n, a in gains.items()})
        last = i == depth - 1
        hp, kp, vp, sp = _layer(hp, p_prompt[i], 0, None, None, None, wts, gf, last)
        hs, ks, vs, ss = _layer(hs, p_sample[i], past, cache_k[i], cache_v[i], state_pool[i], wts, gf, last)
        for lst, val in zip(outs, (kp, vp, sp, ks, vs, ss)):
            lst.append(val)
    return (hp, hs) + tuple(jnp.stack(o) for o in outs)
```
